```python
import math
import jax
import jax.numpy as jnp
from jax import lax
import numpy as np

D_MODEL = 2048
BATCH = 4
SEQ = 2048
DEPTH = 4
DEC_BATCH = 8
DEC_SEQ = 4
PAST_LEN = 16384
PAGE_SIZE = 128

H_SB = 8
DH_SB = D_MODEL // 16
H_MB = 8
DH_MB = D_MODEL // 16
H_RET = 8
DK_RET = D_MODEL // 16
DV_RET = D_MODEL // 16
W_SB = H_SB * DH_SB
W_MB = H_MB * DH_MB
W_RQ = H_RET * DK_RET
W_RV = H_RET * DV_RET
SB_Q_BLOCK = 128
MOBA_BLOCK = 256
MOBA_TOPK = 3
MOBA_Q_BLOCK = 64
ROPE_THETA = 500000.0
ROT_DIM = DH_MB // 4
RET_THETA = 10000.0
RET_CHUNK = 128
RET_DECAY_BASE = 5.0
D_FF = -(-(8 * D_MODEL) // (3 * 256)) * 256
IN_WIDTH = 3 * W_SB + 3 * W_MB + 2 * W_RQ + 2 * W_RV + 3 * D_MODEL
NORM_EPS = 1e-6
F32 = jnp.float32

kernel_name = "hybrid_stickbreak_moba_retention_step"


def rms_norm(x, g):
    xf = x.astype(F32)
    y = xf * lax.rsqrt(jnp.mean(xf * xf, axis=-1, keepdims=True) + NORM_EPS)
    return (y * g.astype(F32)).astype(x.dtype)


def rotary(x, pos, rot_dim, theta):
    half = rot_dim // 2
    inv = jnp.exp(jnp.arange(half, dtype=F32) * (-2.0 * math.log(theta) / rot_dim))
    ang = pos.astype(F32)[:, None] * inv[None, :]
    c = jnp.cos(ang)[:, None, :]
    s = jnp.sin(ang)[:, None, :]
    xr = x[..., :rot_dim].astype(F32)
    x1, x2 = xr[..., :half], xr[..., half:]
    rot = jnp.concatenate([x1 * c - x2 * s, x1 * s + x2 * c], axis=-1).astype(x.dtype)
    return jnp.concatenate([rot, x[..., rot_dim:]], axis=-1)


def mix_project(h, w_in, pos):
    B, T, _ = h.shape
    z = jnp.einsum('btd,de->bte', h, w_in)
    sizes = [W_SB] * 3 + [W_MB] * 3 + [W_RQ, W_RQ, W_RV, W_RV] + [D_MODEL] * 3
    cuts = [int(c) for c in np.cumsum(sizes)[:-1]]
    qa, ka, va, qm, km, vm, qr, kr, vr, gr, ga, gm, gc = jnp.split(z, cuts, axis=-1)
    qa = qa.reshape(B, T, H_SB, DH_SB)
    ka = ka.reshape(B, T, H_SB, DH_SB)
    va = va.reshape(B, T, H_SB, DH_SB)
    qm = rotary(qm.reshape(B, T, H_MB, DH_MB), pos, ROT_DIM, ROPE_THETA)
    km = rotary(km.reshape(B, T, H_MB, DH_MB), pos, ROT_DIM, ROPE_THETA)
    vm = vm.reshape(B, T, H_MB, DH_MB)
    qr = rotary(qr.reshape(B, T, H_RET, DK_RET), pos, DK_RET, RET_THETA)
    kr = rotary(kr.reshape(B, T, H_RET, DK_RET), pos, DK_RET, RET_THETA) * (DK_RET ** -0.5)
    vr = vr.reshape(B, T, H_RET, DV_RET)
    return qa, ka, va, qm, km, vm, qr, kr, vr, gr, ga, gm, gc


def sb_attend(q, k, v, q_pos, k_pos):
    d = q.shape[-1]
    z = jnp.einsum('bqhd,bkhd->bhqk', q, k, preferred_element_type=F32) * (d ** -0.5)
    mask = k_pos[None, :] < q_pos[:, None]
    log_1mb = jnp.where(mask, jax.nn.log_sigmoid(-z), 0.0)
    tail = lax.cumsum(log_1mb, axis=3, reverse=True) - log_1mb
    a = jnp.where(mask, jnp.exp(jax.nn.log_sigmoid(z) + tail), 0.0)
    return jnp.einsum('bhqk,bkhd->bqhd', a.astype(v.dtype), v, preferred_element_type=F32)


def sb_prompt(q, k, v):
    B, S, H, d = q.shape
    nb = S // SB_Q_BLOCK
    qb = q.reshape(B, nb, SB_Q_BLOCK, H, d).transpose(1, 0, 2, 3, 4)
    k_pos = jnp.arange(S, dtype=jnp.int32)

    def block(args):
        qi, i = args
        q_pos = i * SB_Q_BLOCK + jnp.arange(SB_Q_BLOCK, dtype=jnp.int32)
        return sb_attend(qi, k, v, q_pos, k_pos)

    o = lax.map(block, (qb, jnp.arange(nb, dtype=jnp.int32)))
    return o.transpose(1, 0, 2, 3, 4).reshape(B, S, H, d)


def moba_blocks(k, v):
    B, L, H, d = k.shape
    nb = max(-(-L // MOBA_BLOCK), MOBA_TOPK)
    pad = ((0, 0), (0, nb * MOBA_BLOCK - L), (0, 0), (0, 0))
    kb = jnp.pad(k, pad).reshape(B, nb, MOBA_BLOCK, H, d)
    vb = jnp.pad(v, pad).reshape(B, nb, MOBA_BLOCK, H, d)
    kmean = jnp.mean(kb, axis=2, dtype=F32)
    return kb, vb, kmean


def moba_attend(q, q_pos, kb, vb, kmean):
    B, Tq, H, d = q.shape
    NB, BLK = kb.shape[1], kb.shape[2]
    own = q_pos // BLK
    gate = jnp.einsum('bqhd,bnhd->bhqn', q, kmean, preferred_element_type=F32)
    past = jnp.arange(NB, dtype=jnp.int32)[None, :] < own[:, None]
    gate = jnp.where(past, gate, -jnp.inf)
    _, top = lax.top_k(gate, MOBA_TOPK)
    top = top.astype(jnp.int32)
    top_ok = top < own[:, None]
    blocks = jnp.concatenate([top, jnp.broadcast_to(own[:, None], (B, H, Tq, 1)).astype(jnp.int32)], axis=-1)
    bi = jnp.arange(B, dtype=jnp.int32)[:, None, None, None]
    hi = jnp.arange(H, dtype=jnp.int32)[None, :, None, None]
    kg = kb[bi, blocks, :, hi]
    vg = vb[bi, blocks, :, hi]
    kpos = blocks[..., None] * BLK + jnp.arange(BLK, dtype=jnp.int32)
    sel_ok = jnp.concatenate([top_ok, jnp.ones((B, H, Tq, 1), dtype=bool)], axis=-1)
    ok = sel_ok[..., None] & (kpos <= q_pos[:, None, None])
    s = jnp.einsum('bqhd,bhqnkd->bhqnk', q, kg, preferred_element_type=F32) * (d ** -0.5)
    s = jnp.where(ok, s, -jnp.inf)
    p = jax.nn.softmax(s.reshape(B, H, Tq, -1), axis=-1).reshape(s.shape)
    return jnp.einsum('bhqnk,bhqnkd->bqhd', p.astype(vg.dtype), vg, preferred_element_type=F32)


def moba_prompt(q, k, v):
    B, S, H, d = q.shape
    kb, vb, kmean = moba_blocks(k, v)
    nq = S // MOBA_Q_BLOCK
    qb = q.reshape(B, nq, MOBA_Q_BLOCK, H, d).transpose(1, 0, 2, 3, 4)

    def block(args):
        qi, i = args
        q_pos = i * MOBA_Q_BLOCK + jnp.arange(MOBA_Q_BLOCK, dtype=jnp.int32)
        return moba_attend(qi, q_pos, kb, vb, kmean)

    o = lax.map(block, (qb, jnp.arange(nq, dtype=jnp.int32)))
    return o.transpose(1, 0, 2, 3, 4).reshape(B, S, H, d)


def retention_chunk(state, q, k, v, log_gamma):
    T = q.shape[1]
    idx = jnp.arange(T, dtype=F32)
    diff = idx[:, None] - idx[None, :]
    decay = jnp.where(diff >= 0, jnp.exp(log_gamma[:, None, None] * jnp.maximum(diff, 0.0)), 0.0)
    scores = jnp.einsum('bqhd,bkhd->bhqk', q, k, preferred_element_type=F32) * decay[None]
    o = jnp.einsum('bhqk,bkhe->bqhe', scores, v.astype(F32))
    q_decay = jnp.exp((idx[:, None] + 1.0) * log_gamma[None, :])
    o = o + jnp.einsum('bqhd,bhde->bqhe', q.astype(F32), state) * q_decay[None, :, :, None]
    k_decay = jnp.exp((T - 1.0 - idx)[:, None] * log_gamma[None, :])
    new_state = (jnp.exp(T * log_gamma)[None, :, None, None] * state
                 + jnp.einsum('bkhd,bkhe->bhde', k.astype(F32) * k_decay[None, :, :, None], v.astype(F32)))
    return o, new_state


def retention_prompt(q, k, v, log_gamma):
    B, S, H, dk = q.shape
    dv = v.shape[-1]
    nc = S // RET_CHUNK

    def chunks(t):
        return t.reshape(B, nc, RET_CHUNK, H, t.shape[-1]).transpose(1, 0, 2, 3, 4)

    def step(s, xs):
        qc, kc, vc = xs
        o, s_new = retention_chunk(s, qc, kc, vc, log_gamma)
        return s_new, o

    s0 = jnp.zeros((B, H, dk, dv), F32)
    s_fin, o = lax.scan(step, s0, (chunks(q), chunks(k), chunks(v)))
    return o.transpose(1, 0, 2, 3, 4).reshape(B, S, H, dv), s_fin


def mix_merge(oa, om, oc, gr, ga, gm, gc, g_ret, w_pa, w_pm, w_pr, w_o):
    B, T = oa.shape[:2]
    dt = ga.dtype
    cf = oc.astype(F32)
    mu = jnp.mean(cf, axis=-1, keepdims=True)
    var = jnp.mean(jnp.square(cf - mu), axis=-1, keepdims=True)
    cn = ((cf - mu) * lax.rsqrt(var + NORM_EPS)).reshape(B, T, -1) * g_ret.astype(F32)
    c_in = (jax.nn.silu(gr.astype(F32)) * cn).astype(dt)
    ya = jnp.einsum('btw,wd->btd', oa.reshape(B, T, -1).astype(dt), w_pa)
    ym = jnp.einsum('btw,wd->btd', om.reshape(B, T, -1).astype(dt), w_pm)
    yc = jnp.einsum('btw,wd->btd', c_in, w_pr)
    mixed = jax.nn.sigmoid(ga) * ya + jax.nn.sigmoid(gm) * ym + jax.nn.sigmoid(gc) * yc
    return jnp.einsum('btd,de->bte', mixed, w_o)


def swiglu(h, w_gate_up, w_down):
    g, u = jnp.split(jnp.einsum('btd,df->btf', h, w_gate_up), 2, axis=-1)
    return jnp.einsum('btf,fd->btd', jax.nn.silu(g) * u, w_down)


def gather_pages(pool, layer, page_table):
    pages = pool[layer, page_table]
    DB, NP, P, H, d = pages.shape
    return pages.reshape(DB, NP * P, H, d)


def setup_inputs(seed: int = 0) -> dict:
    key = jax.random.key(seed)
    ks = jax.random.split(key, 24)
    n_pages = PAST_LEN // PAGE_SIZE
    n_used = DEC_BATCH * n_pages
    n_pool = n_used + -(-n_used // 4)

    def normal(k, shape, scale):
        return jax.random.normal(k, shape, F32) * scale

    def gain(k, shape):
        return 1.0 + 0.02 * jax.random.normal(k, shape, F32)

    page_table = jax.random.permutation(ks[0], n_pool)[:n_used].reshape(DEC_BATCH, n_pages).astype(jnp.int32)
    return {
        "x_prompt": normal(ks[1], (BATCH, SEQ, D_MODEL), 1.0),
        "x_sample": normal(ks[2], (DEC_BATCH, DEC_SEQ, D_MODEL), 1.0),
        "cache_sb_k": normal(ks[3], (DEPTH, n_pool, PAGE_SIZE, H_SB, DH_SB), 1.0),
        "cache_sb_v": normal(ks[4], (DEPTH, n_pool, PAGE_SIZE, H_SB, DH_SB), 1.0),
        "cache_moba_k": normal(ks[5], (DEPTH, n_pool, PAGE_SIZE, H_MB, DH_MB), 1.0),
        "cache_moba_v": normal(ks[6], (DEPTH, n_pool, PAGE_SIZE, H_MB, DH_MB), 1.0),
        "state_ret": normal(ks[7], (DEPTH, DEC_BATCH, H_RET, DK_RET, DV_RET), 0.5),
        "page_table": page_table,
        "w_in": normal(ks[8], (DEPTH, D_MODEL, IN_WIDTH), D_MODEL ** -0.5),
        "w_proj_sb": normal(ks[9], (DEPTH, W_SB, D_MODEL), W_SB ** -0.5),
        "w_proj_moba": normal(ks[10], (DEPTH, W_MB, D_MODEL), W_MB ** -0.5),
        "w_proj_ret": normal(ks[11], (DEPTH, W_RV, D_MODEL), W_RV ** -0.5),
        "w_out": normal(ks[12], (DEPTH, D_MODEL, D_MODEL), D_MODEL ** -0.5),
        "g_ret_norm": gain(ks[13], (DEPTH, W_RV)),
        "g_mix": gain(ks[14], (DEPTH, D_MODEL)),
        "g_ffn": gain(ks[15], (DEPTH, D_MODEL)),
        "w_gate_up": normal(ks[16], (DEPTH, D_MODEL, 2 * D_FF), D_MODEL ** -0.5),
        "w_down": normal(ks[17], (DEPTH, D_FF, D_MODEL), D_FF ** -0.5),
        "g_final": gain(ks[18], (D_MODEL,)),
    }


def reference(x_prompt, x_sample, cache_sb_k, cache_sb_v, cache_moba_k, cache_moba_v, state_ret, page_table,
              w_in, w_proj_sb, w_proj_moba, w_proj_ret, w_out, g_ret_norm, g_mix, g_ffn, w_gate_up, w_down, g_final):
    S = x_prompt.shape[1]
    Ts = x_sample.shape[1]
    past_len = page_table.shape[1] * cache_sb_k.shape[2]
    pos_p = jnp.arange(S, dtype=jnp.int32)
    pos_s = past_len + jnp.arange(Ts, dtype=jnp.int32)
    pos_all = jnp.arange(past_len + Ts, dtype=jnp.int32)
    log_gamma = jnp.log1p(-jnp.exp2(-RET_DECAY_BASE - jnp.arange(H_RET, dtype=F32)))

    xp, xs = x_prompt, x_sample
    sbk_p, sbv_p, mbk_p, mbv_p, ret_p = [], [], [], [], []
    sbk_s, sbv_s, mbk_s, mbv_s, ret_s = [], [], [], [], []
    for l in range(DEPTH):
        hp = rms_norm(xp, g_mix[l])
        qa, ka, va, qm, km, vm, qr, kr, vr, gr, ga, gm, gc = mix_project(hp, w_in[l], pos_p)
        oa = sb_prompt(qa, ka, va).astype(xp.dtype)
        om = moba_prompt(qm, km, vm).astype(xp.dtype)
        oc, s_fin = retention_prompt(qr, kr, vr, log_gamma)
        xp = xp + mix_merge(oa, om, oc, gr, ga, gm, gc, g_ret_norm[l], w_proj_sb[l], w_proj_moba[l], w_proj_ret[l], w_out[l])
        xp = xp + swiglu(rms_norm(xp, g_ffn[l]), w_gate_up[l], w_down[l])
        sbk_p.append(ka)
        sbv_p.append(va)
        mbk_p.append(km)
        mbv_p.append(vm)
        ret_p.append(s_fin)

        hs = rms_norm(xs, g_mix[l])
        qa, ka, va, qm, km, vm, qr, kr, vr, gr, ga, gm, gc = mix_project(hs, w_in[l], pos_s)
        ka_all = jnp.concatenate([gather_pages(cache_sb_k, l, page_table), ka], axis=1)
        va_all = jnp.concatenate([gather_pages(cache_sb_v, l, page_table), va], axis=1)
        oa = sb_attend(qa, ka_all, va_all, pos_s, pos_all).astype(xs.dtype)
        km_all = jnp.concatenate([gather_pages(cache_moba_k, l, page_table), km], axis=1)
        vm_all = jnp.concatenate([gather_pages(cache_moba_v, l, page_table), vm], axis=1)
        kb, vb, kmean = moba_blocks(km_all, vm_all)
        om = moba_attend(qm, pos_s, kb, vb, kmean).astype(xs.dtype)
        oc, s_new = retention_chunk(state_ret[l].astype(F32), qr, kr, vr, log_gamma)
        xs = xs + mix_merge(oa, om, oc, gr, ga, gm, gc, g_ret_norm[l], w_proj_sb[l], w_proj_moba[l], w_proj_ret[l], w_out[l])
        xs = xs + swiglu(rms_norm(xs, g_ffn[l]), w_gate_up[l], w_down[l])
        sbk_s.append(ka)
        sbv_s.append(va)
        mbk_s.append(km)
        mbv_s.append(vm)
        ret_s.append(s_new)

    y_prompt = rms_norm(xp, g_final)
    y_sample = rms_norm(xs, g_final)
    return (y_prompt, y_sample,
            jnp.stack(sbk_p), jnp.stack(sbv_p), jnp.stack(mbk_p), jnp.stack(mbv_p), jnp.stack(ret_p),
            jnp.stack(sbk_s), jnp.stack(sbv_s), jnp.stack(mbk_s), jnp.stack(mbv_s), jnp.stack(ret_s))
```

```python
import functools
import math

import jax
import jax.numpy as jnp
import numpy as np
from jax import lax
from jax.experimental import pallas as pl
from jax.experimental.pallas import tpu as pltpu

F32 = jnp.float32
BF16 = jnp.bfloat16

N_HEADS = 8
D_HEAD = 128
MOBA_BLOCK = 256
MOBA_TOPK = 3
RET_CHUNK = 128
ROPE_THETA = 500000.0
ROT_DIM = D_HEAD // 4
RET_THETA = 10000.0
RET_DECAY_BASE = 5.0
NORM_EPS = 1e-6
DEC_PAD = 8

LANES = 128
V7X_VMEM_LIMIT_BYTES = 56 * 1024 * 1024

W_HEADS = N_HEADS * D_HEAD
SEG = {name: i * N_HEADS for i, name in enumerate(
    ["qa", "ka", "va", "qm", "km", "vm", "qr", "kr", "vr", "gr"])}
GATE_COL0 = 10 * W_HEADS


def _cparams(dims):
    return pltpu.CompilerParams(dimension_semantics=dims, vmem_limit_bytes=V7X_VMEM_LIMIT_BYTES)


def _sigmoid(x):
    return 1.0 / (1.0 + jnp.exp(-x))


def _dot(a, b):
    return jnp.dot(a, b, preferred_element_type=F32)


def _dot_nt(a, b):
    return lax.dot_general(a, b, (((1,), (1,)), ((), ())), preferred_element_type=F32)


def _dot_tn(a, b):
    return lax.dot_general(a, b, (((0,), (0,)), ((), ())), preferred_element_type=F32)


def _rms_to_bf16(x, g):
    ms = jnp.mean(x * x, axis=-1, keepdims=True)
    return ((x * lax.rsqrt(ms + NORM_EPS)) * g).astype(BF16)


def _inproj_kernel(x_ref, g_ref, w_ref, mc_ref, ms1_ref, ms2_ref, rc_ref, rs_ref, z_ref, h_ref, *, kr_scale):
    j = pl.program_id(1)

    @pl.when(j == 0)
    def _():
        h_ref[...] = _rms_to_bf16(x_ref[...], g_ref[...])

    acc = _dot(h_ref[...], w_ref[...])
    seg = j
    is_moba = (seg == SEG["qm"] // N_HEADS) | (seg == SEG["km"] // N_HEADS)
    is_ret = (seg == SEG["qr"] // N_HEADS) | (seg == SEG["kr"] // N_HEADS)

    @pl.when(is_moba)
    def _():
        c, s1, s2 = mc_ref[...], ms1_ref[...], ms2_ref[...]
        half = ROT_DIM // 2
        for h in range(N_HEADS):
            blk = acc[:, h * D_HEAD:(h + 1) * D_HEAD]
            z_ref[:, h * D_HEAD:(h + 1) * D_HEAD] = (
                blk * c + pltpu.roll(blk, D_HEAD - half, 1) * s1 + pltpu.roll(blk, half, 1) * s2)

    @pl.when(is_ret)
    def _():
        c, s = rc_ref[...], rs_ref[...]
        scale = jnp.where(seg == SEG["kr"] // N_HEADS, kr_scale, 1.0).astype(F32)
        for h in range(N_HEADS):
            blk = acc[:, h * D_HEAD:(h + 1) * D_HEAD]
            z_ref[:, h * D_HEAD:(h + 1) * D_HEAD] = (blk * c + pltpu.roll(blk, D_HEAD // 2, 1) * s) * scale

    @pl.when(jnp.logical_not(is_moba | is_ret))
    def _():
        z_ref[...] = acc


def _inproj(x, g, w, tabs, tm):
    M, D = x.shape
    N = w.shape[1]
    tn = W_HEADS
    tab_spec = pl.BlockSpec((tm, D_HEAD), lambda i, j: (i, 0))
    return pl.pallas_call(
        functools.partial(_inproj_kernel, kr_scale=D_HEAD ** -0.5),
        grid=(M // tm, N // tn),
        in_specs=[pl.BlockSpec((tm, D), lambda i, j: (i, 0)),
                  pl.BlockSpec((1, D), lambda i, j: (0, 0)),
                  pl.BlockSpec((D, tn), lambda i, j: (0, j))] + [tab_spec] * 5,
        out_specs=pl.BlockSpec((tm, tn), lambda i, j: (i, j)),
        out_shape=jax.ShapeDtypeStruct((M, N), F32),
        scratch_shapes=[pltpu.VMEM((tm, D), BF16)],
        compiler_params=_cparams(("parallel", "arbitrary")),
        name="inproj",
    )(x, g.reshape(1, D), w, *tabs)


def _rotary_tables(pos):
    posf = pos.astype(F32)
    half = ROT_DIM // 2
    inv = jnp.exp(jnp.arange(half, dtype=F32) * (-2.0 * math.log(ROPE_THETA) / ROT_DIM))
    ang = posf[:, None] * inv[None, :]
    c, s = jnp.cos(ang), jnp.sin(ang)
    n = pos.shape[0]
    ones = jnp.ones((n, D_HEAD - ROT_DIM), F32)
    zeros_h = jnp.zeros((n, half), F32)
    zeros_r = jnp.zeros((n, D_HEAD - ROT_DIM), F32)
    mc = jnp.concatenate([c, c, ones], axis=1)
    ms1 = jnp.concatenate([-s, zeros_h, zeros_r], axis=1)
    ms2 = jnp.concatenate([zeros_h, s, zeros_r], axis=1)
    half_r = D_HEAD // 2
    inv_r = jnp.exp(jnp.arange(half_r, dtype=F32) * (-2.0 * math.log(RET_THETA) / D_HEAD))
    ang_r = posf[:, None] * inv_r[None, :]
    cr, sr = jnp.cos(ang_r), jnp.sin(ang_r)
    rc = jnp.concatenate([cr, cr], axis=1)
    rs = jnp.concatenate([-sr, sr], axis=1)
    return mc, ms1, ms2, rc, rs


def _sb_block(q, kk, vv, mask, cy, u, scale):
    s = _dot_nt(q, kk) * scale
    sp = jnp.log1p(jnp.exp(-jnp.abs(s)))
    ls = jnp.minimum(s, 0.0) - sp
    lm = jnp.minimum(-s, 0.0) - sp
    if mask is not None:
        lm = jnp.where(mask, lm, 0.0)
    hi = lm.astype(BF16)
    lo = (lm - hi.astype(F32)).astype(BF16)
    tail = _dot(hi, u) + _dot(lo, u)
    a = jnp.exp(ls + tail + cy)
    if mask is not None:
        a = jnp.where(mask, a, 0.0)
    o = _dot(a.astype(BF16), vv)
    return o, cy + tail[:, 0:1] + lm[:, 0:1]


def _strict_upper_ones(n):
    r = lax.broadcasted_iota(jnp.int32, (n, n), 0)
    c = lax.broadcasted_iota(jnp.int32, (n, n), 1)
    return jnp.where(r > c, 1.0, 0.0).astype(BF16)


def _sb_prompt_kernel(q_ref, k_ref, v_ref, o_ref, kb_ref, vb_ref, *, tq, tk, scale):
    S = q_ref.shape[0]
    kb_ref[...] = k_ref[...].astype(BF16)
    vb_ref[...] = v_ref[...].astype(BF16)
    u = _strict_upper_ones(tk)
    row = lax.broadcasted_iota(jnp.int32, (tq, tk), 0)
    col = lax.broadcasted_iota(jnp.int32, (tq, tk), 1)

    def qblock(i, _):
        q0 = pl.multiple_of(i * tq, tq)
        q = q_ref[pl.ds(q0, tq), :].astype(BF16)
        nkb = (q0 + tq) // tk

        def kblock(jj, carry):
            acc, cy = carry
            k0 = pl.multiple_of((nkb - 1 - jj) * tk, tk)
            mask = (k0 + col) < (q0 + row)
            o, cy = _sb_block(q, kb_ref[pl.ds(k0, tk), :], vb_ref[pl.ds(k0, tk), :], mask, cy, u, scale)
            return acc + o, cy

        acc, _ = lax.fori_loop(0, nkb, kblock, (jnp.zeros((tq, D_HEAD), F32), jnp.zeros((tq, 1), F32)))
        o_ref[pl.ds(q0, tq), :] = acc.astype(o_ref.dtype)
        return 0

    lax.fori_loop(0, S // tq, qblock, 0)


def _sb_prompt(z, B, S):
    M = z.shape[0]
    tq = tk = min(256, S)
    blk = lambda seg: pl.BlockSpec((S, D_HEAD), lambda b, h: (b, SEG[seg] + h))
    return pl.pallas_call(
        functools.partial(_sb_prompt_kernel, tq=tq, tk=tk, scale=D_HEAD ** -0.5),
        grid=(B, N_HEADS),
        in_specs=[blk("qa"), blk("ka"), blk("va")],
        out_specs=pl.BlockSpec((S, D_HEAD), lambda b, h: (b, h)),
        out_shape=jax.ShapeDtypeStruct((M, W_HEADS), BF16),
        scratch_shapes=[pltpu.VMEM((S, D_HEAD), BF16), pltpu.VMEM((S, D_HEAD), BF16)],
        compiler_params=_cparams(("parallel", "parallel")),
        name="sb_prompt",
    )(z, z, z)


def _topk_select(gate, n_cand, topk):
    lane = lax.broadcasted_iota(jnp.int32, gate.shape, 1)
    cand = lane < n_cand
    g = jnp.where(cand, gate, -jnp.inf)
    rank = jnp.zeros(gate.shape, F32)
    for m in range(n_cand):
        gm = g[:, m:m + 1]
        beats = (gm > g) | ((gm == g) & (lane > m))
        rank = rank + jnp.where(beats, 1.0, 0.0)
    return cand & (rank < topk)


def _moba_prompt_kernel(q_ref, k_ref, v_ref, o_ref, kb_ref, vb_ref, *, blk, topk, scale):
    S = q_ref.shape[0]
    nb = S // blk
    k = k_ref[...]
    kb_ref[...] = k.astype(BF16)
    vb_ref[...] = v_ref[...].astype(BF16)
    kmean = (jnp.sum(k.reshape(nb, blk, D_HEAD), axis=1) * (1.0 / blk)).astype(BF16)
    row = lax.broadcasted_iota(jnp.int32, (blk, blk), 0)
    col = lax.broadcasted_iota(jnp.int32, (blk, blk), 1)
    lane_nb = lax.broadcasted_iota(jnp.int32, (blk, nb), 1)

    for i in range(nb):
        q = q_ref[i * blk:(i + 1) * blk, :].astype(BF16)
        s = _dot_nt(q, kb_ref[i * blk:(i + 1) * blk, :]) * scale
        s = jnp.where(col <= row, s, -jnp.inf)
        m = jnp.max(s, axis=1, keepdims=True)
        p = jnp.exp(s - m)
        l = jnp.sum(p, axis=1, keepdims=True)
        acc = _dot(p.astype(BF16), vb_ref[i * blk:(i + 1) * blk, :])
        if i > 0:
            if i > topk:
                gate = _dot_nt(q, kmean)
                self_f = jnp.where(_topk_select(gate, i, topk), 1.0, 0.0)

            def kblock(n, carry, i=i):
                m, l, acc = carry
                k0 = pl.multiple_of(n * blk, blk)
                s = _dot_nt(q, kb_ref[pl.ds(k0, blk), :]) * scale
                if i > topk:
                    sel_n = jnp.sum(jnp.where(lane_nb == n, self_f, 0.0), axis=1, keepdims=True)
                    s = jnp.where(sel_n > 0.5, s, -jnp.inf)
                m_new = jnp.maximum(m, jnp.max(s, axis=1, keepdims=True))
                alpha = jnp.exp(m - m_new)
                p = jnp.exp(s - m_new)
                l = alpha * l + jnp.sum(p, axis=1, keepdims=True)
                acc = alpha * acc + _dot(p.astype(BF16), vb_ref[pl.ds(k0, blk), :])
                return m_new, l, acc

            m, l, acc = lax.fori_loop(0, i, kblock, (m, l, acc))
        o_ref[i * blk:(i + 1) * blk, :] = (acc / l).astype(o_ref.dtype)


def _moba_prompt(z, B, S):
    M = z.shape[0]
    blk = lambda seg: pl.BlockSpec((S, D_HEAD), lambda b, h: (b, SEG[seg] + h))
    return pl.pallas_call(
        functools.partial(_moba_prompt_kernel, blk=MOBA_BLOCK, topk=MOBA_TOPK, scale=D_HEAD ** -0.5),
        grid=(B, N_HEADS),
        in_specs=[blk("qm"), blk("km"), blk("vm")],
        out_specs=pl.BlockSpec((S, D_HEAD), lambda b, h: (b, h)),
        out_shape=jax.ShapeDtypeStruct((M, W_HEADS), BF16),
        scratch_shapes=[pltpu.VMEM((S, D_HEAD), BF16), pltpu.VMEM((S, D_HEAD), BF16)],
        compiler_params=_cparams(("parallel", "parallel")),
        name="moba_prompt",
    )(z, z, z)


def _ret_kernel(*refs, chunk, nchunks, has_init):
    if has_init:
        q_ref, k_ref, v_ref, gr_ref, dec_ref, qd_ref, kd_ref, gt_ref, gn_ref, s0_ref, c_ref, st_ref = refs
        state0 = s0_ref[0, 0]
    else:
        q_ref, k_ref, v_ref, gr_ref, dec_ref, qd_ref, kd_ref, gt_ref, gn_ref, c_ref, st_ref = refs
        state0 = jnp.zeros((D_HEAD, D_HEAD), F32)
    decay = dec_ref[0]
    qdec = qd_ref[0]
    kdec = kd_ref[0]
    gt = gt_ref[0]
    gain = gn_ref[...]

    def step(c, state):
        r0 = pl.multiple_of(c * chunk, chunk)
        q = q_ref[pl.ds(r0, chunk), :].astype(BF16)
        k = k_ref[pl.ds(r0, chunk), :]
        vb = v_ref[pl.ds(r0, chunk), :].astype(BF16)
        sc = _dot_nt(q, k.astype(BF16)) * decay
        o = _dot(sc.astype(BF16), vb) + _dot(q, state.astype(BF16)) * qdec
        new_state = gt * state + _dot_tn((k * kdec).astype(BF16), vb)
        mu = jnp.mean(o, axis=-1, keepdims=True)
        d = o - mu
        var = jnp.mean(d * d, axis=-1, keepdims=True)
        cn = (d * lax.rsqrt(var + NORM_EPS)) * gain
        gr = gr_ref[pl.ds(r0, chunk), :]
        c_ref[pl.ds(r0, chunk), :] = ((gr * _sigmoid(gr)) * cn).astype(c_ref.dtype)
        return new_state

    st_ref[0, 0] = lax.fori_loop(0, nchunks, step, state0)


def _ret_tables(n_valid, chunk):
    h = jnp.arange(N_HEADS, dtype=F32)
    log_gamma = jnp.log1p(-jnp.exp2(-RET_DECAY_BASE - h))
    idx = jnp.arange(chunk, dtype=F32)
    valid = idx < n_valid
    diff = idx[:, None] - idx[None, :]
    decay = jnp.where(diff >= 0, jnp.exp(log_gamma[:, None, None] * jnp.maximum(diff, 0.0)), 0.0)
    decay = jnp.where(valid[None, :, None] & valid[None, None, :], decay, 0.0)
    qd = jnp.exp((idx[None, :] + 1.0) * log_gamma[:, None])
    kd = jnp.where(valid[None, :], jnp.exp((n_valid - 1.0 - idx)[None, :] * log_gamma[:, None]), 0.0)
    gt = jnp.exp(n_valid * log_gamma)
    bc = lambda t: jnp.broadcast_to(t[..., None], t.shape + (D_HEAD,))
    return decay, bc(qd), bc(kd), bc(gt[:, None])


def _retention(z, g_ret, Bn, T, n_valid, chunk, state0=None):
    M = z.shape[0]
    decay, qd, kd, gt = _ret_tables(n_valid, chunk)
    blk = lambda seg: pl.BlockSpec((T, D_HEAD), lambda b, h: (b, SEG[seg] + h))
    per_head = lambda r: pl.BlockSpec((1, r, D_HEAD), lambda b, h: (h, 0, 0))
    st_spec = pl.BlockSpec((1, 1, D_HEAD, D_HEAD), lambda b, h: (b, h, 0, 0))
    in_specs = [blk("qr"), blk("kr"), blk("vr"), blk("gr"),
                pl.BlockSpec((1, chunk, chunk), lambda b, h: (h, 0, 0)),
                per_head(chunk), per_head(chunk), per_head(1),
                pl.BlockSpec((1, D_HEAD), lambda b, h: (0, h))]
    args = [z, z, z, z, decay, qd, kd, gt, g_ret.reshape(1, W_HEADS)]
    if state0 is not None:
        in_specs.append(st_spec)
        args.append(state0)
    return pl.pallas_call(
        functools.partial(_ret_kernel, chunk=chunk, nchunks=T // chunk, has_init=state0 is not None),
        grid=(Bn, N_HEADS),
        in_specs=in_specs,
        out_specs=[pl.BlockSpec((T, D_HEAD), lambda b, h: (b, h)), st_spec],
        out_shape=[jax.ShapeDtypeStruct((M, W_HEADS), BF16),
                   jax.ShapeDtypeStruct((Bn, N_HEADS, D_HEAD, D_HEAD), F32)],
        compiler_params=_cparams(("parallel", "parallel")),
        name="retention",
    )(*args)


def _merge_kernel(oa_ref, om_ref, oc_ref, wa_ref, wm_ref, wr_ref, ga_ref, gm_ref, gc_ref, o_ref):
    ya = _dot(oa_ref[...], wa_ref[...])
    ym = _dot(om_ref[...], wm_ref[...])
    yc = _dot(oc_ref[...], wr_ref[...])
    mixed = _sigmoid(ga_ref[...]) * ya + _sigmoid(gm_ref[...]) * ym + _sigmoid(gc_ref[...]) * yc
    o_ref[...] = mixed.astype(o_ref.dtype)


def _merge(oa, om, oc, wa, wm, wr, z, tm):
    M, W = oa.shape
    D = wa.shape[1]
    tn = min(1024, D)
    a_spec = pl.BlockSpec((tm, W), lambda i, j: (i, 0))
    w_spec = pl.BlockSpec((W, tn), lambda i, j: (0, j))
    gate = lambda g: pl.BlockSpec((tm, tn), lambda i, j: (i, (GATE_COL0 + g * D) // tn + j))
    return pl.pallas_call(
        _merge_kernel,
        grid=(M // tm, D // tn),
        in_specs=[a_spec] * 3 + [w_spec] * 3 + [gate(0), gate(1), gate(2)],
        out_specs=pl.BlockSpec((tm, tn), lambda i, j: (i, j)),
        out_shape=jax.ShapeDtypeStruct((M, D), BF16),
        compiler_params=_cparams(("parallel", "parallel")),
        name="merge",
    )(oa, om, oc, wa, wm, wr, z, z, z)


def _mm_res_kernel(a_ref, w_ref, r_ref, o_ref):
    o_ref[...] = r_ref[...] + _dot(a_ref[...], w_ref[...])


def _mm_res(a, w, res, tm, tn):
    M, K = a.shape
    N = w.shape[1]
    return pl.pallas_call(
        _mm_res_kernel,
        grid=(M // tm, N // tn),
        in_specs=[pl.BlockSpec((tm, K), lambda i, j: (i, 0)),
                  pl.BlockSpec((K, tn), lambda i, j: (0, j)),
                  pl.BlockSpec((tm, tn), lambda i, j: (i, j))],
        out_specs=pl.BlockSpec((tm, tn), lambda i, j: (i, j)),
        out_shape=jax.ShapeDtypeStruct((M, N), F32),
        compiler_params=_cparams(("parallel", "parallel")),
        name="mm_res",
    )(a, w, res)


def _ffn_up_kernel(x_ref, g_ref, wg_ref, wu_ref, o_ref, h_ref):
    @pl.when(pl.program_id(1) == 0)
    def _():
        h_ref[...] = _rms_to_bf16(x_ref[...], g_ref[...])

    h = h_ref[...]
    g = _dot(h, wg_ref[...])
    u = _dot(h, wu_ref[...])
    o_ref[...] = ((g * _sigmoid(g)) * u).astype(o_ref.dtype)


def _ffn_up(x, g, w_gu, tm, tn):
    M, D = x.shape
    F = w_gu.shape[1] // 2
    nf = F // tn
    return pl.pallas_call(
        _ffn_up_kernel,
        grid=(M // tm, nf),
        in_specs=[pl.BlockSpec((tm, D), lambda i, j: (i, 0)),
                  pl.BlockSpec((1, D), lambda i, j: (0, 0)),
                  pl.BlockSpec((D, tn), lambda i, j: (0, j)),
                  pl.BlockSpec((D, tn), lambda i, j: (0, nf + j))],
        out_specs=pl.BlockSpec((tm, tn), lambda i, j: (i, j)),
        out_shape=jax.ShapeDtypeStruct((M, F), BF16),
        scratch_shapes=[pltpu.VMEM((tm, D), BF16)],
        compiler_params=_cparams(("parallel", "arbitrary")),
        name="ffn_up",
    )(x, g.reshape(1, D), w_gu, w_gu)


def _rms_kernel(x_ref, g_ref, o_ref):
    x = x_ref[...]
    ms = jnp.mean(x * x, axis=-1, keepdims=True)
    o_ref[...] = (x * lax.rsqrt(ms + NORM_EPS)) * g_ref[...]


def _final_norm(x, g, tm):
    M, D = x.shape
    return pl.pallas_call(
        _rms_kernel,
        grid=(M // tm,),
        in_specs=[pl.BlockSpec((tm, D), lambda i: (i, 0)), pl.BlockSpec((1, D), lambda i: (0, 0))],
        out_specs=pl.BlockSpec((tm, D), lambda i: (i, 0)),
        out_shape=jax.ShapeDtypeStruct((M, D), F32),
        compiler_params=_cparams(("parallel",)),
        name="final_norm",
    )(x, g.reshape(1, D))


def _head_rows_queries(q):
    R = N_HEADS * DEC_PAD
    tiled = jnp.concatenate([q] * N_HEADS, axis=0)
    rh = lax.broadcasted_iota(jnp.int32, (R, W_HEADS), 0) // DEC_PAD
    ch = lax.broadcasted_iota(jnp.int32, (R, W_HEADS), 1) // D_HEAD
    return jnp.where(rh == ch, tiled, 0.0).astype(BF16)


def _head_diag(acc):
    ch = lax.broadcasted_iota(jnp.int32, (DEC_PAD, W_HEADS), 1) // D_HEAD
    out = jnp.zeros((DEC_PAD, W_HEADS), F32)
    for h in range(N_HEADS):
        out = out + jnp.where(ch == h, acc[h * DEC_PAD:(h + 1) * DEC_PAD, :], 0.0)
    return out


def _pad_rows(x, n):
    return jnp.concatenate([x, jnp.zeros((n - x.shape[0], x.shape[1]), x.dtype)], axis=0)


def _page_rows(cache):
    return cache.reshape(-1, D_HEAD)


def _load_page(ref, page):
    return jnp.concatenate([ref[pl.ds(h, page, stride=N_HEADS), :].astype(BF16) for h in range(N_HEADS)], axis=1)


def _sb_decode_kernel(pt_ref, q_ref, kn_ref, vn_ref, kc_ref, vc_ref, o_ref, qh_ref, cy_ref, acc_ref, *, page, scale):
    p = pl.program_id(1)
    R = N_HEADS * DEC_PAD
    u = _strict_upper_ones(page)

    @pl.when(p == 0)
    def _():
        qh = _head_rows_queries(q_ref[0])
        qh_ref[...] = qh
        kk = _pad_rows(kn_ref[0].astype(BF16), page)
        vv = _pad_rows(vn_ref[0].astype(BF16), page)
        t = lax.broadcasted_iota(jnp.int32, (R, page), 0) % DEC_PAD
        j = lax.broadcasted_iota(jnp.int32, (R, page), 1)
        o, cy = _sb_block(qh, kk, vv, j < t, jnp.zeros((R, 1), F32), u, scale)
        acc_ref[...] = o
        cy_ref[...] = jnp.broadcast_to(cy, cy_ref.shape)

    o, cy = _sb_block(qh_ref[...], _load_page(kc_ref, page), _load_page(vc_ref, page), None, cy_ref[:, 0:1], u, scale)
    acc_ref[...] += o
    cy_ref[...] = jnp.broadcast_to(cy, cy_ref.shape)

    @pl.when(p == pl.num_programs(1) - 1)
    def _():
        o_ref[0] = _head_diag(acc_ref[...]).astype(o_ref.dtype)


def _sb_decode(zs3, cache_k, cache_v, pt_flat, layer, n_pages):
    DB = zs3.shape[0]
    n_pool, page = cache_k.shape[1], cache_k.shape[2]
    ck, cv = _page_rows(cache_k), _page_rows(cache_v)
    zblk = lambda seg: pl.BlockSpec((1, DEC_PAD, W_HEADS), lambda b, p, pt: (b, 0, SEG[seg] // N_HEADS))
    cblk = pl.BlockSpec((page * N_HEADS, D_HEAD),
                        lambda b, p, pt: (layer * n_pool + pt[b * n_pages + n_pages - 1 - p], 0))
    R = N_HEADS * DEC_PAD
    return pl.pallas_call(
        functools.partial(_sb_decode_kernel, page=page, scale=D_HEAD ** -0.5),
        grid_spec=pltpu.PrefetchScalarGridSpec(
            num_scalar_prefetch=1,
            grid=(DB, n_pages),
            in_specs=[zblk("qa"), zblk("ka"), zblk("va"), cblk, cblk],
            out_specs=pl.BlockSpec((1, DEC_PAD, W_HEADS), lambda b, p, pt: (b, 0, 0)),
            scratch_shapes=[pltpu.VMEM((R, W_HEADS), BF16), pltpu.VMEM((R, LANES), F32),
                            pltpu.VMEM((R, W_HEADS), F32)]),
        out_shape=jax.ShapeDtypeStruct((DB, DEC_PAD, W_HEADS), BF16),
        compiler_params=_cparams(("parallel", "arbitrary")),
        name="sb_decode",
    )(pt_flat, zs3, zs3, zs3, ck, cv)


def _kmean_kernel(pt_ref, k0_ref, k1_ref, o_ref, *, page, inv):
    s = (jnp.sum(k0_ref[...].reshape(page, N_HEADS, D_HEAD), axis=0)
         + jnp.sum(k1_ref[...].reshape(page, N_HEADS, D_HEAD), axis=0))
    o_ref[0, 0] = s * inv


def _moba_kmean(cache_k, pt_flat, layer, DB, n_pages):
    n_pool, page = cache_k.shape[1], cache_k.shape[2]
    ck = _page_rows(cache_k)
    ppb = MOBA_BLOCK // page
    assert ppb == 2
    nb = n_pages // ppb
    cblk = lambda r: pl.BlockSpec((page * N_HEADS, D_HEAD),
                                  lambda b, n, pt: (layer * n_pool + pt[b * n_pages + ppb * n + r], 0))
    out = pl.pallas_call(
        functools.partial(_kmean_kernel, page=page, inv=1.0 / MOBA_BLOCK),
        grid_spec=pltpu.PrefetchScalarGridSpec(
            num_scalar_prefetch=1,
            grid=(DB, nb),
            in_specs=[cblk(0), cblk(1)],
            out_specs=pl.BlockSpec((1, 1, N_HEADS, D_HEAD), lambda b, n, pt: (b, n, 0, 0))),
        out_shape=jax.ShapeDtypeStruct((DB, nb, N_HEADS, D_HEAD), F32),
        compiler_params=_cparams(("parallel", "parallel")),
        name="moba_kmean",
    )(pt_flat, ck, ck)
    return out.reshape(DB, nb, W_HEADS)


def _moba_decode_kernel(pt_ref, q_ref, kn_ref, vn_ref, km_ref, kc_ref, vc_ref, o_ref,
                        qh_ref, sel_ref, m_ref, l_ref, acc_ref, *, page, ppb, topk, scale):
    p = pl.program_id(1)
    R = N_HEADS * DEC_PAD
    nb = km_ref.shape[1]

    def update(s, vv, first):
        if first:
            m_new = jnp.max(s, axis=1, keepdims=True)
            pr = jnp.exp(s - m_new)
            l_new = jnp.sum(pr, axis=1, keepdims=True)
            acc_ref[...] = _dot(pr.astype(BF16), vv)
        else:
            m_old = m_ref[:, 0:1]
            m_new = jnp.maximum(m_old, jnp.max(s, axis=1, keepdims=True))
            alpha = jnp.exp(m_old - m_new)
            pr = jnp.exp(s - m_new)
            l_new = alpha * l_ref[:, 0:1] + jnp.sum(pr, axis=1, keepdims=True)
            acc_ref[...] = alpha * acc_ref[...] + _dot(pr.astype(BF16), vv)
        m_ref[...] = jnp.broadcast_to(m_new, m_ref.shape)
        l_ref[...] = jnp.broadcast_to(l_new, l_ref.shape)

    @pl.when(p == 0)
    def _():
        qh = _head_rows_queries(q_ref[0])
        qh_ref[...] = qh
        gate = _dot_nt(qh, km_ref[0].astype(BF16))
        sel_ref[...] = jnp.where(_topk_select(gate, nb, topk), 1.0, 0.0).astype(BF16)
        kk = _pad_rows(kn_ref[0].astype(BF16), page)
        vv = _pad_rows(vn_ref[0].astype(BF16), page)
        t = lax.broadcasted_iota(jnp.int32, (R, page), 0) % DEC_PAD
        j = lax.broadcasted_iota(jnp.int32, (R, page), 1)
        s = jnp.where(j <= t, _dot_nt(qh, kk) * scale, -jnp.inf)
        update(s, vv, True)

    n = p // ppb
    onehot = jnp.where(lax.broadcasted_iota(jnp.int32, (nb, page), 0) == n, 1.0, 0.0).astype(BF16)
    sel = _dot(sel_ref[...], onehot)
    s = _dot_nt(qh_ref[...], _load_page(kc_ref, page)) * scale
    s = jnp.where(sel > 0.5, s, -jnp.inf)
    update(s, _load_page(vc_ref, page), False)

    @pl.when(p == pl.num_programs(1) - 1)
    def _():
        o_ref[0] = _head_diag(acc_ref[...] / l_ref[:, 0:1]).astype(o_ref.dtype)


def _moba_decode(zs3, kmean, cache_k, cache_v, pt_flat, layer, n_pages):
    DB = zs3.shape[0]
    n_pool, page = cache_k.shape[1], cache_k.shape[2]
    ck, cv = _page_rows(cache_k), _page_rows(cache_v)
    nb = kmean.shape[1]
    zblk = lambda seg: pl.BlockSpec((1, DEC_PAD, W_HEADS), lambda b, p, pt: (b, 0, SEG[seg] // N_HEADS))
    cblk = pl.BlockSpec((page * N_HEADS, D_HEAD), lambda b, p, pt: (layer * n_pool + pt[b * n_pages + p], 0))
    R = N_HEADS * DEC_PAD
    return pl.pallas_call(
        functools.partial(_moba_decode_kernel, page=page, ppb=MOBA_BLOCK // page, topk=MOBA_TOPK,
                          scale=D_HEAD ** -0.5),
        grid_spec=pltpu.PrefetchScalarGridSpec(
            num_scalar_prefetch=1,
            grid=(DB, n_pages),
            in_specs=[zblk("qm"), zblk("km"), zblk("vm"),
                      pl.BlockSpec((1, nb, W_HEADS), lambda b, p, pt: (b, 0, 0)), cblk, cblk],
            out_specs=pl.BlockSpec((1, DEC_PAD, W_HEADS), lambda b, p, pt: (b, 0, 0)),
            scratch_shapes=[pltpu.VMEM((R, W_HEADS), BF16), pltpu.VMEM((R, nb), BF16),
                            pltpu.VMEM((R, LANES), F32), pltpu.VMEM((R, LANES), F32),
                            pltpu.VMEM((R, W_HEADS), F32)]),
        out_shape=jax.ShapeDtypeStruct((DB, DEC_PAD, W_HEADS), BF16),
        compiler_params=_cparams(("parallel", "arbitrary")),
        name="moba_decode",
    )(pt_flat, zs3, zs3, zs3, kmean, ck, cv)


def _dense_tail(x, z, oa, om, oc, w, tm):
    mixed = _merge(oa, om, oc, w["pa"], w["pm"], w["pr"], z, tm)
    x = _mm_res(mixed, w["o"], x, tm, min(1024, x.shape[1]))
    act = _ffn_up(x, w["g_ffn"], w["gu"], tm, 512)
    return _mm_res(act, w["down"], x, tm, 512)


def _heads(z2, seg, lead):
    c0 = SEG[seg] * D_HEAD
    return z2[:, c0:c0 + W_HEADS].reshape(*lead, N_HEADS, D_HEAD)


def kernel(x_prompt, x_sample, cache_sb_k, cache_sb_v, cache_moba_k, cache_moba_v, state_ret, page_table, w_in, w_proj_sb, w_proj_moba, w_proj_ret, w_out, g_ret_norm, g_mix, g_ffn, w_gate_up, w_down, g_final):
    B, S, D = x_prompt.shape
    DB, Ts, _ = x_sample.shape
    depth = w_in.shape[0]
    n_pages = page_table.shape[1]
    page = cache_sb_k.shape[2]
    past_len = n_pages * page
    assert D == 16 * D_HEAD and Ts <= DEC_PAD and page == LANES and S % MOBA_BLOCK == 0
    assert past_len % MOBA_BLOCK == 0 and (past_len + Ts - 1) // MOBA_BLOCK == past_len // MOBA_BLOCK

    tm_p = min(512, B * S)
    tm_s = DB * DEC_PAD
    xp = x_prompt.reshape(B * S, D)
    xs = jnp.pad(x_sample, ((0, 0), (0, DEC_PAD - Ts), (0, 0))).reshape(DB * DEC_PAD, D)
    tabs_p = _rotary_tables(jnp.tile(jnp.arange(S, dtype=jnp.int32), B))
    tabs_s = _rotary_tables(jnp.tile(past_len + jnp.arange(DEC_PAD, dtype=jnp.int32), DB))
    pt_flat = page_table.reshape(-1).astype(jnp.int32)

    outs = [[] for _ in range(10)]
    for l in range(depth):
        w = dict(pa=w_proj_sb[l].astype(BF16), pm=w_proj_moba[l].astype(BF16), pr=w_proj_ret[l].astype(BF16),
                 o=w_out[l].astype(BF16), gu=w_gate_up[l].astype(BF16), down=w_down[l].astype(BF16),
                 g_ffn=g_ffn[l])
        w_in_l = w_in[l].astype(BF16)

        z = _inproj(xp, g_mix[l], w_in_l, tabs_p, tm_p)
        oa = _sb_prompt(z, B, S)
        om = _moba_prompt(z, B, S)
        oc, st_p = _retention(z, g_ret_norm[l], B, S, RET_CHUNK, RET_CHUNK)
        xp = _dense_tail(xp, z, oa, om, oc, w, tm_p)
        for i, seg in enumerate(["ka", "va", "km", "vm"]):
            outs[i].append(_heads(z, seg, (B, S)))
        outs[4].append(st_p)

        zs = _inproj(xs, g_mix[l], w_in_l, tabs_s, tm_s)
        zs3 = zs.reshape(DB, DEC_PAD, zs.shape[1])
        oa = _sb_decode(zs3, cache_sb_k, cache_sb_v, pt_flat, l, n_pages).reshape(DB * DEC_PAD, W_HEADS)
        kmean = _moba_kmean(cache_moba_k, pt_flat, l, DB, n_pages)
        om = _moba_decode(zs3, kmean, cache_moba_k, cache_moba_v, pt_flat, l, n_pages).reshape(DB * DEC_PAD, W_HEADS)
        oc, st_s = _retention(zs, g_ret_norm[l], DB, DEC_PAD, Ts, DEC_PAD, state0=state_ret[l])
        xs = _dense_tail(xs, zs, oa, om, oc, w, tm_s)
        for i, seg in enumerate(["ka", "va", "km", "vm"]):
            outs[5 + i].append(_heads(zs, seg, (DB, DEC_PAD))[:, :Ts])
        outs[9].append(st_s)

    y_prompt = _final_norm(xp, g_final, tm_p).reshape(B, S, D)
    y_sample = _final_norm(xs, g_final, tm_s).reshape(DB, DEC_PAD, D)[:, :Ts]
    stacked = [jnp.stack(o) for o in outs]
    return (y_prompt, y_sample, *stacked)
```
